```python
import math
import jax, jax.numpy as jnp
from jax import lax
import numpy as np

D_MODEL = 2048
BATCH = 32
SEQ = 256
DEPTH = 2
DEC_BATCH = 2
DEC_SEQ = 1024
PAST_LEN = 512

GRID_W = 64
N_EVEN = (DEPTH + 1) // 2
N_ODD = DEPTH // 2
NORM_EPS = 1e-6
NEG_INF = -1e30
NA_HEADS = 16
NA_HEAD_DIM = 64
NA_WIDTH = NA_HEADS * NA_HEAD_DIM
NA_WIN_ROWS = 8
NA_WIN_COLS = 16
NA_QC = 16
NA_KC = 32
CTX_Q_BLOCK = 128
HY_WIDTH = D_MODEL - NA_WIDTH
HY_BANDS = 16
HY_EMB = 1 + 2 * HY_BANDS
HY_FILT_HIDDEN = 64
HY_DECAY_TARGET = 1e-2
HY_SHORT_DECAY_PCT = 0.3
HY_LONG_DECAY_PCT = 1.5
HY_SHIFT = 0.05
FN_WIDTH = D_MODEL // 2
FN_GROUPS = 4
FN_GROUP = FN_WIDTH // FN_GROUPS
POOL_WINDOWS = (2, 4, 8, 16)
POOL_WIDTH = D_MODEL - FN_WIDTH
POOL_GROUP = POOL_WIDTH // len(POOL_WINDOWS)
MOE_GROUPS = 4
MOE_EXPERTS_PER_GROUP = 4
MOE_EXPERTS = MOE_GROUPS * MOE_EXPERTS_PER_GROUP
MOE_TOP_K = 2
MOE_FF = D_MODEL // 4

kernel_name = 'hybrid_natten_hyena_fnet_pool_hmoe_step'


def _rmsnorm(x, g):
    x32 = x.astype(jnp.float32)
    y = x32 * lax.rsqrt(jnp.mean(jnp.square(x32), axis=-1, keepdims=True) + NORM_EPS)
    return y.astype(x.dtype) * g


def _modulation(cvec, w, b):
    m = jax.nn.silu(cvec) @ w + b
    return [t[:, None, :] for t in jnp.split(m, 6, axis=-1)]


def _ctx_attention(q, k, v):
    b, lc, h, dh = q.shape
    nb = lc // CTX_Q_BLOCK
    qb = jnp.moveaxis(q.reshape(b, nb, CTX_Q_BLOCK, h, dh), 1, 0)
    scale = dh ** -0.5

    def block(qi):
        s = jnp.einsum('bqhd,bkhd->bhqk', qi, k).astype(jnp.float32) * scale
        p = jax.nn.softmax(s, axis=-1).astype(v.dtype)
        return jnp.einsum('bhqk,bkhd->bqhd', p, v)

    o = lax.map(block, qb)
    return jnp.moveaxis(o, 0, 1).reshape(b, lc, h * dh)


def _neigh_attention(q, k, v, k_ctx, v_ctx, rpb):
    b, seq_len, h, dh = q.shape
    rows = seq_len // GRID_W
    wr = min(NA_WIN_ROWS, rows)
    ncb = GRID_W // NA_QC
    nk = wr * NA_KC
    r = np.arange(rows)
    rs = np.clip(r - wr // 2, 0, rows - wr)
    key_rows = rs[:, None] + np.arange(wr)[None, :]
    j = np.arange(ncb)
    kc0 = np.clip(j * NA_QC - NA_WIN_COLS // 2, 0, GRID_W - NA_KC)
    key_cols = kc0[:, None] + np.arange(NA_KC)[None, :]
    q_cols = j[:, None] * NA_QC + np.arange(NA_QC)[None, :]
    wc0 = np.clip(q_cols - NA_WIN_COLS // 2, 0, GRID_W - NA_WIN_COLS)
    kcol = key_cols[:, None, :]
    col_ok = (kcol >= wc0[..., None]) & (kcol < wc0[..., None] + NA_WIN_COLS)
    mask = np.broadcast_to(col_ok[:, :, None, :], (ncb, NA_QC, wr, NA_KC)).reshape(ncb, 1, NA_QC, nk)
    dr = key_rows - r[:, None] + NA_WIN_ROWS - 1
    dc = np.clip(kcol - q_cols[..., None] + NA_WIN_COLS - 1, 0, 2 * NA_WIN_COLS - 2)
    bias = rpb[:, dr[:, None, None, :, None], dc[None, :, :, None, :]]
    bias = bias.reshape(h, rows, ncb, NA_QC, nk).transpose(1, 2, 0, 3, 4).astype(jnp.float32)
    qb = q.reshape(b, rows, ncb, NA_QC, h, dh)
    gr = key_rows[:, None, :, None]
    gc = key_cols[None, :, None, :]
    kg = k.reshape(b, rows, GRID_W, h, dh)[:, gr, gc].reshape(b, rows, ncb, nk, h, dh)
    vg = v.reshape(b, rows, GRID_W, h, dh)[:, gr, gc].reshape(b, rows, ncb, nk, h, dh)
    scale = dh ** -0.5
    s_loc = jnp.einsum('brjqhd,brjkhd->brjhqk', qb, kg).astype(jnp.float32) * scale + bias
    s_loc = jnp.where(mask, s_loc, NEG_INF)
    s_ctx = jnp.einsum('brjqhd,bkhd->brjhqk', qb, k_ctx).astype(jnp.float32) * scale
    p = jax.nn.softmax(jnp.concatenate([s_loc, s_ctx], axis=-1), axis=-1).astype(v.dtype)
    o = (jnp.einsum('brjhqk,brjkhd->brjqhd', p[..., :nk], vg)
         + jnp.einsum('brjhqk,bkhd->brjqhd', p[..., nk:], v_ctx))
    return o.reshape(b, seq_len, h * dh)


def _centred_conv3(u, w, b):
    up = jnp.pad(u, ((0, 0), (1, 1), (0, 0)))
    return up[:, :-2] * w[0] + up[:, 1:-1] * w[1] + up[:, 2:] * w[2] + b


def _hyena_filter(seq_len, w1, b1, freq, w2, b2, w3):
    t = jnp.linspace(0.0, 1.0, seq_len, dtype=jnp.float32)[:, None]
    omega = 2.0 * math.pi * jnp.arange(seq_len, dtype=jnp.float32)[:, None] / seq_len
    bands = jnp.linspace(1e-4, HY_BANDS - 1, HY_BANDS, dtype=jnp.float32)[None, :]
    z = jnp.concatenate([t, jnp.cos(bands * omega), -jnp.sin(bands * omega)], axis=-1)
    hdn = jnp.sin(freq * (z @ w1 + b1))
    hdn = jnp.sin(freq * (hdn @ w2 + b2))
    filt = (hdn @ w3).astype(jnp.float32).reshape(seq_len, 2, HY_WIDTH)
    max_decay = math.log(HY_DECAY_TARGET) / HY_SHORT_DECAY_PCT
    min_decay = math.log(HY_DECAY_TARGET) / HY_LONG_DECAY_PCT
    deltas = jnp.abs(jnp.linspace(min_decay, max_decay, HY_WIDTH, dtype=jnp.float32))
    window = jnp.exp(-t * deltas[None, :]) + HY_SHIFT
    filt = filt * window[:, None, :]
    fwd, bwd = filt[:, 0], filt[:, 1]
    k2 = jnp.concatenate([fwd, jnp.zeros((1, HY_WIDTH), jnp.float32), bwd[:0:-1]], axis=0)
    return k2 * lax.rsqrt(jnp.sum(jnp.square(k2), axis=0, keepdims=True) + NORM_EPS)


def _bidir_fftconv(u, k2):
    seq_len = u.shape[1]
    uf = jnp.fft.rfft(u.astype(jnp.float32), n=2 * seq_len, axis=1)
    kf = jnp.fft.rfft(k2, n=2 * seq_len, axis=0)
    return jnp.fft.irfft(uf * kf[None], n=2 * seq_len, axis=1)[:, :seq_len]


def _hyena(z, short_w, short_b, d_bias, w1, b1, freq, w2, b2, w3):
    z = _centred_conv3(z, short_w, short_b)
    v, x1, x0 = jnp.split(z, 3, axis=-1)
    u = v * x1
    k2 = _hyena_filter(z.shape[1], w1, b1, freq, w2, b2, w3)
    y = _bidir_fftconv(u, k2).astype(z.dtype) + d_bias * u
    return y * x0


def _even_proj(h, w_in):
    p = h @ w_in
    b, seq_len, _ = h.shape
    q, k, v = [t.reshape(b, seq_len, NA_HEADS, NA_HEAD_DIM) for t in jnp.split(p[..., :3 * NA_WIDTH], 3, axis=-1)]
    return q, k, v, p[..., 3 * NA_WIDTH:]


def _odd_mixer(h, w_in, fn_lin, pool_lin, pool_scale, w_out):
    p = h @ w_in
    b, seq_len, _ = p.shape
    uf = p[..., :FN_WIDTH].reshape(b, seq_len, FN_GROUPS, FN_GROUP).astype(jnp.float32)
    f = jnp.fft.fft2(uf, axes=(1, 3), norm='ortho').real.astype(p.dtype)
    f = jnp.einsum('blgc,gcd->blgd', f, fn_lin).reshape(b, seq_len, FN_WIDTH)
    up = p[..., FN_WIDTH:].astype(jnp.float32)
    cs = jnp.concatenate([jnp.zeros((b, 1, POOL_WIDTH), jnp.float32), jnp.cumsum(up, axis=1)], axis=1)
    pos = np.arange(seq_len)
    pooled = []
    for g, w in enumerate(POOL_WINDOWS):
        lo = np.clip(pos - w // 2, 0, seq_len)
        hi = np.clip(pos + w // 2, 0, seq_len)
        sl = slice(g * POOL_GROUP, (g + 1) * POOL_GROUP)
        cnt = (hi - lo).astype(np.float32)[None, :, None]
        pooled.append((cs[:, hi, sl] - cs[:, lo, sl]) / cnt - up[..., sl])
    pm = jnp.stack(pooled, axis=2).astype(p.dtype)
    pm = jnp.einsum('blgc,gcd->blgd', pm, pool_lin).reshape(b, seq_len, POOL_WIDTH) * pool_scale
    return jnp.concatenate([f, pm], axis=-1) @ w_out


def _hier_moe(h, w_group, b_group, w_expert, b_expert, w1, w3, w2):
    shape = h.shape
    t = h.reshape(-1, D_MODEL)
    n_tok = t.shape[0]
    g_prob = jax.nn.softmax((t @ w_group).astype(jnp.float32) + b_group, axis=-1)
    g_p, g_idx = lax.top_k(g_prob, 1)
    e_logits = ((t @ w_expert).astype(jnp.float32) + b_expert).reshape(n_tok, MOE_GROUPS, MOE_EXPERTS_PER_GROUP)
    e_sel = jnp.take_along_axis(e_logits, g_idx[:, :, None], axis=1)[:, 0]
    e_val, e_idx = lax.top_k(e_sel, MOE_TOP_K)
    e_w = jax.nn.softmax(e_val, axis=-1) * g_p
    ids = g_idx * MOE_EXPERTS_PER_GROUP + e_idx
    combine = jnp.einsum('tk,tke->te', e_w, jax.nn.one_hot(ids, MOE_EXPERTS, dtype=jnp.float32))
    hid = jax.nn.silu(jnp.einsum('td,edf->tef', t, w1)) * jnp.einsum('td,edf->tef', t, w3)
    hid = hid * combine[:, :, None].astype(hid.dtype)
    return jnp.einsum('tef,efd->td', hid, w2).reshape(shape)


def setup_inputs(seed: int = 0) -> dict:
    key = jax.random.key(seed)
    keys = iter(jax.random.split(key, 48))
    D = D_MODEL

    def nrm(shape, scale):
        return jax.random.normal(next(keys), shape, jnp.float32) * scale

    return {
        'x_prompt': nrm((BATCH, SEQ, D), 1.0),
        'x_sample': nrm((DEC_BATCH, DEC_SEQ, D), 1.0),
        'c': nrm((DEC_BATCH, D), 1.0),
        'cache_k': nrm((DEC_BATCH, N_EVEN, PAST_LEN, NA_HEADS, NA_HEAD_DIM), 1.0),
        'cache_v': nrm((DEC_BATCH, N_EVEN, PAST_LEN, NA_HEADS, NA_HEAD_DIM), 1.0),
        'c_ctx': nrm((D,), 1.0),
        'ada_w': nrm((DEPTH, D, 6 * D), 0.5 * D ** -0.5),
        'ada_b': nrm((DEPTH, 6 * D), 0.02),
        'norm_mix': 1.0 + nrm((DEPTH, D), 0.1),
        'norm_ffn': 1.0 + nrm((DEPTH, D), 0.1),
        'norm_final': 1.0 + nrm((D,), 0.1),
        'w_in_even': nrm((N_EVEN, D, 3 * NA_WIDTH + 3 * HY_WIDTH), D ** -0.5),
        'w_out_even': nrm((N_EVEN, D, D), D ** -0.5),
        'na_rpb': nrm((N_EVEN, NA_HEADS, 2 * NA_WIN_ROWS - 1, 2 * NA_WIN_COLS - 1), 0.1),
        'hy_short_w': nrm((N_EVEN, 3, 3 * HY_WIDTH), 3 ** -0.5),
        'hy_short_b': nrm((N_EVEN, 3 * HY_WIDTH), 0.02),
        'hy_filt_w1': nrm((N_EVEN, HY_EMB, HY_FILT_HIDDEN), HY_EMB ** -0.5),
        'hy_filt_b1': nrm((N_EVEN, HY_FILT_HIDDEN), 0.1),
        'hy_filt_freq': 1.0 + nrm((N_EVEN, HY_FILT_HIDDEN), 0.1),
        'hy_filt_w2': nrm((N_EVEN, HY_FILT_HIDDEN, HY_FILT_HIDDEN), HY_FILT_HIDDEN ** -0.5),
        'hy_filt_b2': nrm((N_EVEN, HY_FILT_HIDDEN), 0.1),
        'hy_filt_w3': nrm((N_EVEN, HY_FILT_HIDDEN, 2 * HY_WIDTH), HY_FILT_HIDDEN ** -0.5),
        'hy_bias_d': nrm((N_EVEN, HY_WIDTH), 0.5),
        'w_in_odd': nrm((N_ODD, D, FN_WIDTH + POOL_WIDTH), D ** -0.5),
        'fn_lin': nrm((N_ODD, FN_GROUPS, FN_GROUP, FN_GROUP), FN_GROUP ** -0.5),
        'pool_lin': nrm((N_ODD, len(POOL_WINDOWS), POOL_GROUP, POOL_GROUP), POOL_GROUP ** -0.5),
        'pool_scale': 1.0 + nrm((N_ODD, POOL_WIDTH), 0.1),
        'w_out_odd': nrm((N_ODD, D, D), D ** -0.5),
        'moe_w_group': nrm((DEPTH, D, MOE_GROUPS), D ** -0.5),
        'moe_b_group': nrm((DEPTH, MOE_GROUPS), 0.01),
        'moe_w_expert': nrm((DEPTH, D, MOE_EXPERTS), D ** -0.5),
        'moe_b_expert': nrm((DEPTH, MOE_EXPERTS), 0.01),
        'moe_w1': nrm((DEPTH, MOE_EXPERTS, D, MOE_FF), D ** -0.5),
        'moe_w3': nrm((DEPTH, MOE_EXPERTS, D, MOE_FF), D ** -0.5),
        'moe_w2': nrm((DEPTH, MOE_EXPERTS, MOE_FF, D), MOE_FF ** -0.5),
    }


def reference(x_prompt, x_sample, c, cache_k, cache_v, c_ctx, ada_w, ada_b, norm_mix, norm_ffn, norm_final,
              w_in_even, w_out_even, na_rpb, hy_short_w, hy_short_b, hy_filt_w1, hy_filt_b1, hy_filt_freq,
              hy_filt_w2, hy_filt_b2, hy_filt_w3, hy_bias_d, w_in_odd, fn_lin, pool_lin, pool_scale, w_out_odd,
              moe_w_group, moe_b_group, moe_w_expert, moe_b_expert, moe_w1, moe_w3, moe_w2):
    xp, xs = x_prompt, x_sample
    new_k, new_v = [], []
    for l in range(DEPTH):
        mp = _modulation(c_ctx[None, :], ada_w[l], ada_b[l])
        ms = _modulation(c, ada_w[l], ada_b[l])
        hp = _rmsnorm(xp, norm_mix[l]) * (1.0 + mp[1]) + mp[0]
        hs = _rmsnorm(xs, norm_mix[l]) * (1.0 + ms[1]) + ms[0]
        if l % 2 == 0:
            e = l // 2
            filt = (hy_filt_w1[e], hy_filt_b1[e], hy_filt_freq[e], hy_filt_w2[e], hy_filt_b2[e], hy_filt_w3[e])
            qp, kp, vp, zp = _even_proj(hp, w_in_even[e])
            ap = _ctx_attention(qp, kp, vp)
            bp = _hyena(zp, hy_short_w[e], hy_short_b[e], hy_bias_d[e], *filt)
            yp = jnp.concatenate([ap, bp], axis=-1) @ w_out_even[e]
            new_k.append(kp)
            new_v.append(vp)
            qs, ks, vs, zs = _even_proj(hs, w_in_even[e])
            a_s = _neigh_attention(qs, ks, vs, cache_k[:, e], cache_v[:, e], na_rpb[e])
            b_s = _hyena(zs, hy_short_w[e], hy_short_b[e], hy_bias_d[e], *filt)
            ys = jnp.concatenate([a_s, b_s], axis=-1) @ w_out_even[e]
        else:
            o = l // 2
            yp = _odd_mixer(hp, w_in_odd[o], fn_lin[o], pool_lin[o], pool_scale[o], w_out_odd[o])
            ys = _odd_mixer(hs, w_in_odd[o], fn_lin[o], pool_lin[o], pool_scale[o], w_out_odd[o])
        xp = xp + mp[2] * yp
        xs = xs + ms[2] * ys
        moe = (moe_w_group[l], moe_b_group[l], moe_w_expert[l], moe_b_expert[l], moe_w1[l], moe_w3[l], moe_w2[l])
        hp = _rmsnorm(xp, norm_ffn[l]) * (1.0 + mp[4]) + mp[3]
        hs = _rmsnorm(xs, norm_ffn[l]) * (1.0 + ms[4]) + ms[3]
        xp = xp + mp[5] * _hier_moe(hp, *moe)
        xs = xs + ms[5] * _hier_moe(hs, *moe)
    y_prompt = _rmsnorm(xp, norm_final)
    y_sample = _rmsnorm(xs, norm_final)
    new_cache_k = jnp.stack(new_k, axis=1)
    new_cache_v = jnp.stack(new_v, axis=1)
    return (y_prompt, y_sample, new_cache_k, new_cache_v)
```

```python
import functools
import math

import ml_dtypes
import numpy as np

import jax
import jax.numpy as jnp
from jax import lax
from jax.experimental import pallas as pl
from jax.experimental.pallas import tpu as pltpu

F32 = jnp.float32
BF16 = jnp.bfloat16

D = 2048
BATCH = 32
SEQ = 256
DEC_BATCH = 2
DEC_SEQ = 1024
PAST_LEN = 512
TP = BATCH * SEQ
TS = DEC_BATCH * DEC_SEQ
T = TP + TS
GRID_W = 64
GRID_ROWS = DEC_SEQ // GRID_W
NORM_EPS = 1e-6
NEG_INF = -1e30
NA_HEADS = 16
NA_HEAD_DIM = 64
NA_WIDTH = NA_HEADS * NA_HEAD_DIM
NA_WIN_ROWS = 8
NA_WIN_COLS = 16
HY_WIDTH = D - NA_WIDTH
HY_BANDS = 16
HY_EMB = 1 + 2 * HY_BANDS
HY_EMB_PAD = 128
HY_FILT_HIDDEN = 64
HY_HIDDEN_PAD = 128
HY_DECAY_TARGET = 1e-2
HY_SHORT_DECAY_PCT = 0.3
HY_LONG_DECAY_PCT = 1.5
HY_SHIFT = 0.05
FN_WIDTH = D // 2
FN_GROUPS = 4
FN_GROUP = FN_WIDTH // FN_GROUPS
POOL_WINDOWS = (2, 4, 8, 16)
POOL_WIDTH = D - FN_WIDTH
POOL_GROUP = POOL_WIDTH // len(POOL_WINDOWS)
MOE_GROUPS = 4
MOE_EPG = 4
MOE_EXPERTS = MOE_GROUPS * MOE_EPG
MOE_FF = D // 4
N_MOD_ROWS = 8
ROUTER_ROWS = 32

V7X_VMEM_BYTES = 64 * 1024 * 1024
VMEM_LIMIT = V7X_VMEM_BYTES - 8 * 1024 * 1024
ROW_CHUNK = 64

MOE_TM = 256
MOE_SLOTS = 2 * T + MOE_EXPERTS * MOE_TM
MOE_TILES = MOE_SLOTS // MOE_TM


def _cparams(n_axes):
    return pltpu.CompilerParams(dimension_semantics=("arbitrary",) * n_axes, vmem_limit_bytes=VMEM_LIMIT)


def _mod_row(row0):
    return jnp.where(row0 < TP, 0, 1 + (row0 - TP) // DEC_SEQ)


def _split_bf16(x):
    hi = x.astype(BF16)
    lo = (x - hi.astype(F32)).astype(BF16)
    return hi, lo


def _dot(a, b):
    return jnp.dot(a, b, preferred_element_type=F32)


def _dot_nt(a, b):
    return lax.dot_general(a, b, (((1,), (1,)), ((), ())), preferred_element_type=F32)


def _dot_cx(c_hi, c_lo, x):
    x_hi, x_lo = _split_bf16(x)
    return _dot(c_hi, x_hi) + _dot(c_hi, x_lo) + _dot(c_lo, x_hi)


def _dot_xc(x, c_hi, c_lo):
    x_hi, x_lo = _split_bf16(x)
    return _dot(x_hi, c_hi) + _dot(x_lo, c_hi) + _dot(x_hi, c_lo)


def _dot_split(a, b):
    a_hi, a_lo = _split_bf16(a)
    b_hi, b_lo = _split_bf16(b)
    return _dot(a_hi, b_hi) + _dot(a_hi, b_lo) + _dot(a_lo, b_hi)


def _silu(x):
    return x / (1.0 + jnp.exp(-x))


def _np_split(x):
    x32 = np.asarray(x, np.float32)
    hi = x32.astype(ml_dtypes.bfloat16)
    lo = (x32 - hi.astype(np.float32)).astype(ml_dtypes.bfloat16)
    return hi, lo


@functools.lru_cache(maxsize=None)
def _half_angle_dft(n):
    k = np.outer(np.arange(n), np.arange(n)) % (2 * n)
    ang = np.pi * k / n
    return _np_split(np.cos(ang)) + _np_split(np.sin(ang))


@functools.lru_cache(maxsize=None)
def _full_dft(n, scale):
    k = np.outer(np.arange(n), np.arange(n)) % n
    ang = 2.0 * np.pi * k / n
    return _np_split(scale * np.cos(ang)) + _np_split(scale * np.sin(ang))


@functools.lru_cache(maxsize=None)
def _hyena_tables(seq_len):
    t = np.linspace(0.0, 1.0, seq_len)[:, None]
    omega = 2.0 * np.pi * np.arange(seq_len)[:, None] / seq_len
    bands = np.linspace(1e-4, HY_BANDS - 1, HY_BANDS)[None, :]
    z = np.concatenate([t, np.cos(bands * omega), -np.sin(bands * omega)], axis=-1)
    z = np.pad(z, ((0, 0), (0, HY_EMB_PAD - HY_EMB))).astype(np.float32)
    max_decay = math.log(HY_DECAY_TARGET) / HY_SHORT_DECAY_PCT
    min_decay = math.log(HY_DECAY_TARGET) / HY_LONG_DECAY_PCT
    deltas = np.abs(np.linspace(min_decay, max_decay, HY_WIDTH))
    window = (np.exp(-t * deltas[None, :]) + HY_SHIFT).astype(np.float32)
    return z, window


@functools.lru_cache(maxsize=None)
def _pool_tables(seq_len):
    pos = np.arange(seq_len)
    bands, cnts = [], []
    for w in POOL_WINDOWS:
        lo = np.clip(pos - w // 2, 0, seq_len)
        hi = np.clip(pos + w // 2, 0, seq_len)
        s = pos[None, :]
        bands.append(((s >= lo[:, None]) & (s < hi[:, None])).astype(np.float32))
        cnts.append((hi - lo).astype(np.float32)[:, None])
    return np.stack(bands).astype(ml_dtypes.bfloat16), np.stack(cnts)


@functools.lru_cache(maxsize=None)
def _na_index_tables():
    rr = np.arange(NA_WIN_ROWS)[:, None]
    ki = np.arange(NA_WIN_ROWS)[None, :]
    dr = ki - rr + NA_WIN_ROWS - 1
    q = np.arange(GRID_W)[:, None]
    kc = np.arange(GRID_W)[None, :]
    dc = np.clip(kc - q + NA_WIN_COLS - 1, 0, 2 * NA_WIN_COLS - 2)
    wc0 = np.clip(q - NA_WIN_COLS // 2, 0, GRID_W - NA_WIN_COLS)
    valid = (kc >= wc0) & (kc < wc0 + NA_WIN_COLS)
    return dr, dc, valid


def _mod_kernel(ct_ref, w_ref, b_ref, o_ref, s_scr):
    tn = o_ref.shape[-1]
    c = ct_ref[...]
    s_scr[...] = _silu(c)

    def body(k, accs):
        base = pl.multiple_of(k * 32, 32)
        wk = w_ref[pl.ds(base, 32), :]
        sk = s_scr[pl.ds(base, 32), :]
        out = []
        for r in range(1 + DEC_BATCH):
            prod = wk * sk[:, r:r + 1]
            a = accs[r]
            for g in range(4):
                a = a + prod[8 * g:8 * g + 8]
            out.append(a)
        return tuple(out)

    zero = jnp.zeros((8, tn), F32)
    accs = lax.fori_loop(0, D // 32, body, (zero,) * (1 + DEC_BATCH))
    o_ref[...] = jnp.zeros_like(o_ref)
    for r in range(1 + DEC_BATCH):
        o_ref[r:r + 1, :] = jnp.sum(accs[r], axis=0, keepdims=True) + b_ref[...]


def _modulation(cvec_t, ada_w, ada_b):
    depth, _, n_out = ada_w.shape
    tn = 1024
    return pl.pallas_call(
        _mod_kernel,
        grid=(depth, n_out // tn),
        in_specs=[
            pl.BlockSpec((D, N_MOD_ROWS), lambda l, j: (0, 0)),
            pl.BlockSpec((None, D, tn), lambda l, j: (l, 0, j)),
            pl.BlockSpec((None, 1, tn), lambda l, j: (l, 0, j)),
        ],
        out_specs=pl.BlockSpec((None, N_MOD_ROWS, tn), lambda l, j: (l, 0, j)),
        out_shape=jax.ShapeDtypeStruct((depth, N_MOD_ROWS, n_out), F32),
        scratch_shapes=[pltpu.VMEM((D, N_MOD_ROWS), F32)],
        compiler_params=_cparams(2),
        name="modulation",
    )(cvec_t, ada_w, ada_b.reshape(depth, 1, n_out))


def _norm_rows(x, g, scale1, shift):
    inv = lax.rsqrt(jnp.mean(x * x, axis=-1, keepdims=True) + NORM_EPS)
    return ((x * inv) * g) * scale1 + shift


def _norm_mm_kernel(x_ref, g_ref, sc_ref, sh_ref, w_ref, o_ref, h_scr, *, tm):
    i = pl.program_id(0)

    @pl.when(pl.program_id(1) == 0)
    def _():
        row = _mod_row(i * tm)
        scale1 = 1.0 + sc_ref[pl.ds(row, 1), :]
        shift = sh_ref[pl.ds(row, 1), :]
        g = g_ref[...]

        def body(c, carry):
            r0 = pl.multiple_of(c * ROW_CHUNK, ROW_CHUNK)
            h = _norm_rows(x_ref[pl.ds(r0, ROW_CHUNK), :], g, scale1, shift)
            h_scr[pl.ds(r0, ROW_CHUNK), :] = h.astype(BF16)
            return carry

        lax.fori_loop(0, tm // ROW_CHUNK, body, 0)

    o_ref[...] = _dot(h_scr[...], w_ref[...].astype(BF16))


def _norm_matmul(x, g, g_layer, scale, shift, w, w_layer, *, tm=1024, tn=512):
    n_out = w.shape[-1]
    return pl.pallas_call(
        functools.partial(_norm_mm_kernel, tm=tm),
        grid=(T // tm, n_out // tn),
        in_specs=[
            pl.BlockSpec((tm, D), lambda i, j: (i, 0)),
            pl.BlockSpec((None, 1, D), lambda i, j: (g_layer, 0, 0)),
            pl.BlockSpec((N_MOD_ROWS, D), lambda i, j: (0, 0)),
            pl.BlockSpec((N_MOD_ROWS, D), lambda i, j: (0, 0)),
            pl.BlockSpec((None, D, tn), lambda i, j: (w_layer, 0, j)),
        ],
        out_specs=pl.BlockSpec((tm, tn), lambda i, j: (i, j)),
        out_shape=jax.ShapeDtypeStruct((T, n_out), F32),
        scratch_shapes=[pltpu.VMEM((tm, D), BF16)],
        compiler_params=_cparams(2),
        name="norm_matmul",
    )(x, g.reshape(g.shape[0], 1, D), scale, shift, w)


def _proj_res_kernel(m_ref, w_ref, x_ref, gate_ref, o_ref, a_scr, *, tm):
    i = pl.program_id(0)

    @pl.when(pl.program_id(1) == 0)
    def _():
        def body(c, carry):
            r0 = pl.multiple_of(c * ROW_CHUNK, ROW_CHUNK)
            a_scr[pl.ds(r0, ROW_CHUNK), :] = m_ref[pl.ds(r0, ROW_CHUNK), :].astype(BF16)
            return carry

        lax.fori_loop(0, tm // ROW_CHUNK, body, 0)

    y = _dot(a_scr[...], w_ref[...].astype(BF16))
    gate = gate_ref[pl.ds(_mod_row(i * tm), 1), :]
    o_ref[...] = x_ref[...] + gate * y


def _proj_residual(mix, w, layer, x, gate, *, tm=1024, tn=512):
    return pl.pallas_call(
        functools.partial(_proj_res_kernel, tm=tm),
        grid=(T // tm, D // tn),
        in_specs=[
            pl.BlockSpec((tm, D), lambda i, j: (i, 0)),
            pl.BlockSpec((None, D, tn), lambda i, j: (layer, 0, j)),
            pl.BlockSpec((tm, tn), lambda i, j: (i, j)),
            pl.BlockSpec((N_MOD_ROWS, tn), lambda i, j: (0, j)),
        ],
        out_specs=pl.BlockSpec((tm, tn), lambda i, j: (i, j)),
        out_shape=jax.ShapeDtypeStruct((T, D), F32),
        scratch_shapes=[pltpu.VMEM((tm, D), BF16)],
        compiler_params=_cparams(2),
        name="proj_residual",
    )(mix, w, x, gate)


def _final_norm_kernel(x_ref, g_ref, o_ref):
    g = g_ref[...]

    def body(c, carry):
        r0 = pl.multiple_of(c * ROW_CHUNK, ROW_CHUNK)
        x = x_ref[pl.ds(r0, ROW_CHUNK), :]
        inv = lax.rsqrt(jnp.mean(x * x, axis=-1, keepdims=True) + NORM_EPS)
        o_ref[pl.ds(r0, ROW_CHUNK), :] = (x * inv) * g
        return carry

    lax.fori_loop(0, x_ref.shape[0] // ROW_CHUNK, body, 0)


def _final_norm(x, g, row0, n_rows, *, tm=512):
    return pl.pallas_call(
        _final_norm_kernel,
        grid=(n_rows // tm,),
        in_specs=[
            pl.BlockSpec((tm, D), lambda i: (row0 // tm + i, 0)),
            pl.BlockSpec((1, D), lambda i: (0, 0)),
        ],
        out_specs=pl.BlockSpec((tm, D), lambda i: (i, 0)),
        out_shape=jax.ShapeDtypeStruct((n_rows, D), F32),
        compiler_params=_cparams(1),
        name="final_norm",
    )(x, g.reshape(1, D))


def _softmax_pv(scores, values):
    m = functools.reduce(jnp.maximum, [jnp.max(s, axis=-1, keepdims=True) for s in scores])
    ps = [jnp.exp(s - m) for s in scores]
    denom = functools.reduce(jnp.add, [jnp.sum(p, axis=-1, keepdims=True) for p in ps])
    acc = functools.reduce(jnp.add, [_dot(p.astype(BF16), v) for p, v in zip(ps, values)])
    return acc / denom


def _ctx_attn_kernel(q_ref, k_ref, v_ref, o_ref):
    scale = NA_HEAD_DIM ** -0.5
    for h in range(NA_HEADS):
        sl = slice(h * NA_HEAD_DIM, (h + 1) * NA_HEAD_DIM)
        q = q_ref[:, sl].astype(BF16)
        k = k_ref[:, sl].astype(BF16)
        v = v_ref[:, sl].astype(BF16)
        o_ref[:, sl] = _softmax_pv([_dot_nt(q, k) * scale], [v])


def _ctx_attention(qkvz):
    blk = (SEQ, NA_WIDTH)
    return pl.pallas_call(
        _ctx_attn_kernel,
        grid=(BATCH,),
        in_specs=[
            pl.BlockSpec(blk, lambda b: (b, 0)),
            pl.BlockSpec(blk, lambda b: (b, 1)),
            pl.BlockSpec(blk, lambda b: (b, 2)),
        ],
        out_specs=pl.BlockSpec(blk, lambda b: (b, 0)),
        out_shape=jax.ShapeDtypeStruct((T, D), F32),
        compiler_params=_cparams(1),
        name="ctx_attention",
    )(qkvz, qkvz, qkvz)


def _na_window_start(r):
    return jnp.clip(r - NA_WIN_ROWS // 2, 0, GRID_ROWS - NA_WIN_ROWS)


def _na_attn_kernel(q_ref, k_ref, v_ref, kc_ref, vc_ref, bias_ref, mix_ref, o_ref):
    del mix_ref
    scale = NA_HEAD_DIM ** -0.5
    n_loc = NA_WIN_ROWS * GRID_W
    start = pl.multiple_of(_na_window_start(pl.program_id(1)) * GRID_W, GRID_W)
    for h in range(NA_HEADS):
        sl = slice(h * NA_HEAD_DIM, (h + 1) * NA_HEAD_DIM)
        q = q_ref[:, sl].astype(BF16)
        k_loc = k_ref[pl.ds(start, n_loc), sl].astype(BF16)
        v_loc = v_ref[pl.ds(start, n_loc), sl].astype(BF16)
        k_ctx = kc_ref[:, sl].astype(BF16)
        v_ctx = vc_ref[:, sl].astype(BF16)
        s_loc = _dot_nt(q, k_loc) * scale + bias_ref[h]
        s_ctx = _dot_nt(q, k_ctx) * scale
        o_ref[:, sl] = _softmax_pv([s_loc, s_ctx], [v_loc, v_ctx])


def _na_bias_table(rpb):
    dr, dc, valid = _na_index_tables()
    tab = rpb[:, dr[:, None, :, None], dc[None, :, None, :]]
    tab = jnp.where(valid[None, None, :, None, :], tab, NEG_INF)
    return tab.transpose(1, 0, 2, 3, 4).reshape(NA_WIN_ROWS, NA_HEADS, GRID_W, NA_WIN_ROWS * GRID_W)


def _na_attention(qkvz, cache_k, cache_v, rpb, mix):
    bias = _na_bias_table(rpb)
    row_blocks = DEC_SEQ // GRID_W

    def q_map(b, r):
        return (TP // GRID_W + b * row_blocks + r, 0)

    def bias_map(b, r):
        return (r - _na_window_start(r), 0, 0, 0)

    return pl.pallas_call(
        _na_attn_kernel,
        grid=(DEC_BATCH, row_blocks),
        in_specs=[
            pl.BlockSpec((GRID_W, NA_WIDTH), q_map),
            pl.BlockSpec((DEC_SEQ, NA_WIDTH), lambda b, r: (TP // DEC_SEQ + b, 1)),
            pl.BlockSpec((DEC_SEQ, NA_WIDTH), lambda b, r: (TP // DEC_SEQ + b, 2)),
            pl.BlockSpec((None, PAST_LEN, NA_WIDTH), lambda b, r: (b, 0, 0)),
            pl.BlockSpec((None, PAST_LEN, NA_WIDTH), lambda b, r: (b, 0, 0)),
            pl.BlockSpec((None, NA_HEADS, GRID_W, NA_WIN_ROWS * GRID_W), bias_map),
            pl.BlockSpec(memory_space=pl.ANY),
        ],
        out_specs=pl.BlockSpec((GRID_W, NA_WIDTH), q_map),
        out_shape=jax.ShapeDtypeStruct((T, D), F32),
        input_output_aliases={6: 0},
        compiler_params=_cparams(2),
        name="na_attention",
    )(qkvz, qkvz, qkvz, cache_k.reshape(DEC_BATCH, PAST_LEN, NA_WIDTH),
      cache_v.reshape(DEC_BATCH, PAST_LEN, NA_WIDTH), bias, mix)


def _hy_filter_kernel(z_ref, w1_ref, b1_ref, fr_ref, w2_ref, b2_ref, w3f_ref, w3b_ref, win_ref,
                      chi_ref, clo_ref, shi_ref, slo_ref, a_ref, cs_ref, el_ref):
    seq_len = z_ref.shape[0]
    fr = fr_ref[...]
    hdn = jnp.sin(fr * (_dot_split(z_ref[...], w1_ref[...]) + b1_ref[...]))
    hdn = jnp.sin(fr * (_dot_split(hdn, w2_ref[...]) + b2_ref[...]))
    win = win_ref[...]
    fwd = _dot_split(hdn, w3f_ref[...]) * win
    bwd = _dot_split(hdn, w3b_ref[...]) * win
    row = lax.broadcasted_iota(jnp.int32, fwd.shape, 0)
    bwd = jnp.where(row == 0, 0.0, bwd)
    nrm = lax.rsqrt(jnp.sum(fwd * fwd, axis=0, keepdims=True) + jnp.sum(bwd * bwd, axis=0, keepdims=True) + NORM_EPS)
    even = fwd + bwd
    odd = fwd - bwd
    sign = jnp.where(row % 2 == 0, 1.0, -1.0)
    k_cos = _dot_cx(chi_ref[...], clo_ref[...], even)
    k_sin = _dot_cx(shi_ref[...], slo_ref[...], odd)
    k_nyq = jnp.sum(sign * even, axis=0, keepdims=True)
    row_scale = jnp.where(row == 0, 0.5 / seq_len, 1.0 / seq_len)
    a_ref[...] = k_cos * nrm * row_scale
    cs_ref[...] = k_sin * nrm * (1.0 / seq_len)
    el_ref[...] = k_nyq * nrm * (0.5 / seq_len)


def _hyena_filter(seq_len, e, w1, b1, freq, w2, b2, w3, *, ct=256):
    z, window = _hyena_tables(seq_len)
    consts = _half_angle_dft(seq_len)
    hid = HY_HIDDEN_PAD
    pad_h = hid - HY_FILT_HIDDEN
    w1p = jnp.pad(w1[e], ((0, HY_EMB_PAD - HY_EMB), (0, pad_h)))
    b1p = jnp.pad(b1[e], (0, pad_h))[None]
    frp = jnp.pad(freq[e], (0, pad_h))[None]
    w2p = jnp.pad(w2[e], ((0, pad_h), (0, pad_h)))
    b2p = jnp.pad(b2[e], (0, pad_h))[None]
    w3p = jnp.pad(w3[e], ((0, pad_h), (0, 0)))
    full = lambda shape: pl.BlockSpec(shape, lambda c: (0,) * len(shape))
    n_ct = HY_WIDTH // ct
    return pl.pallas_call(
        _hy_filter_kernel,
        grid=(n_ct,),
        in_specs=[
            full((seq_len, HY_EMB_PAD)), full((HY_EMB_PAD, hid)), full((1, hid)), full((1, hid)),
            full((hid, hid)), full((1, hid)),
            pl.BlockSpec((hid, ct), lambda c: (0, c)),
            pl.BlockSpec((hid, ct), lambda c: (0, n_ct + c)),
            pl.BlockSpec((seq_len, ct), lambda c: (0, c)),
        ] + [full((seq_len, seq_len))] * 4,
        out_specs=[
            pl.BlockSpec((seq_len, ct), lambda c: (0, c)),
            pl.BlockSpec((seq_len, ct), lambda c: (0, c)),
            pl.BlockSpec((1, ct), lambda c: (0, c)),
        ],
        out_shape=[
            jax.ShapeDtypeStruct((seq_len, HY_WIDTH), F32),
            jax.ShapeDtypeStruct((seq_len, HY_WIDTH), F32),
            jax.ShapeDtypeStruct((1, HY_WIDTH), F32),
        ],
        compiler_params=_cparams(1),
        name="hyena_filter",
    )(z, w1p, b1p, frp, w2p, b2p, w3p, w3p, window, *consts)


def _hyena_kernel(zv_ref, zx1_ref, zx0_ref, wv_ref, wx1_ref, wx0_ref, bv_ref, bx1_ref, bx0_ref, d_ref,
                  a_ref, cs_ref, el_ref, chi_ref, clo_ref, shi_ref, slo_ref, mix_ref, o_ref):
    del mix_ref
    seq_len = zv_ref.shape[0]
    row = lax.broadcasted_iota(jnp.int32, zv_ref.shape, 0)
    first, last = row == 0, row == seq_len - 1

    def conv3(z_ref, w_ref, b_ref):
        z = z_ref[...]
        prev = jnp.where(first, 0.0, pltpu.roll(z, 1, axis=0))
        nxt = jnp.where(last, 0.0, pltpu.roll(z, seq_len - 1, axis=0))
        return prev * w_ref[0:1, :] + z * w_ref[1:2, :] + nxt * w_ref[2:3, :] + b_ref[...]

    u = conv3(zv_ref, wv_ref, bv_ref) * conv3(zx1_ref, wx1_ref, bx1_ref)
    chi, clo, shi, slo = chi_ref[...], clo_ref[...], shi_ref[...], slo_ref[...]
    u_cos = _dot_cx(chi, clo, u)
    u_sin = _dot_cx(shi, slo, u)
    sign = jnp.where(row % 2 == 0, 1.0, -1.0)
    u_nyq = jnp.sum(sign * u, axis=0, keepdims=True)
    a, cs = a_ref[...], cs_ref[...]
    y = (_dot_cx(chi, clo, u_cos * a - u_sin * cs) + _dot_cx(shi, slo, u_cos * cs + u_sin * a)
         + sign * (u_nyq * el_ref[...]))
    o_ref[...] = (y + d_ref[...] * u) * conv3(zx0_ref, wx0_ref, bx0_ref)


def _hyena(qkvz, seq_len, row_block0, n_seq, ct, e, short_w, short_b, d_bias, filt, mix):
    a, cs, el = filt
    consts = _half_angle_dft(seq_len)
    n_ct = HY_WIDTH // ct
    z_col0 = 3 * NA_WIDTH // ct

    def z_spec(part):
        return pl.BlockSpec((seq_len, ct), lambda b, c: (row_block0 + b, z_col0 + part * n_ct + c))

    def w_spec(part):
        return pl.BlockSpec((None, 3, ct), lambda b, c: (e, 0, part * n_ct + c))

    def b_spec(part):
        return pl.BlockSpec((None, 1, ct), lambda b, c: (e, 0, part * n_ct + c))

    coef = pl.BlockSpec((seq_len, ct), lambda b, c: (0, c))
    vec = pl.BlockSpec((1, ct), lambda b, c: (0, c))
    const = pl.BlockSpec((seq_len, seq_len), lambda b, c: (0, 0))
    short_b3 = short_b.reshape(short_b.shape[0], 1, 3 * HY_WIDTH)
    return pl.pallas_call(
        _hyena_kernel,
        grid=(n_seq, n_ct),
        in_specs=[z_spec(0), z_spec(1), z_spec(2), w_spec(0), w_spec(1), w_spec(2),
                  b_spec(0), b_spec(1), b_spec(2),
                  pl.BlockSpec((None, 1, ct), lambda b, c: (e, 0, c)),
                  coef, coef, vec, const, const, const, const,
                  pl.BlockSpec(memory_space=pl.ANY)],
        out_specs=pl.BlockSpec((seq_len, ct), lambda b, c: (row_block0 + b, NA_WIDTH // ct + c)),
        out_shape=jax.ShapeDtypeStruct((T, D), F32),
        input_output_aliases={17: 0},
        compiler_params=_cparams(2),
        name="hyena",
    )(qkvz, qkvz, qkvz, short_w, short_w, short_w, short_b3, short_b3, short_b3,
      d_bias.reshape(d_bias.shape[0], 1, HY_WIDTH), a, cs, el, *consts, mix)


def _fnet_kernel(u_ref, cs_hi_ref, cs_lo_ref, chi_ref, clo_ref, shi_ref, slo_ref, lin_ref, *rest):
    o_ref = rest[-1]
    ab = _dot_xc(u_ref[...], cs_hi_ref[...], cs_lo_ref[...])
    f = (_dot_cx(chi_ref[...], clo_ref[...], ab[:, :FN_GROUP])
         - _dot_cx(shi_ref[...], slo_ref[...], ab[:, FN_GROUP:]))
    o_ref[...] = _dot(f.astype(BF16), lin_ref[...].astype(BF16))


def _fnet(p, seq_len, row_block0, n_seq, o, fn_lin, mix):
    scale = 1.0 / math.sqrt(seq_len * FN_GROUP)
    c_hi, c_lo, s_hi, s_lo = _full_dft(FN_GROUP, scale)
    cs_hi = np.concatenate([c_hi, s_hi], axis=1)
    cs_lo = np.concatenate([c_lo, s_lo], axis=1)
    seq_consts = _full_dft(seq_len, 1.0)
    const = lambda shape: pl.BlockSpec(shape, lambda b, g: (0, 0))
    blk = pl.BlockSpec((seq_len, FN_GROUP), lambda b, g: (row_block0 + b, g))
    in_specs = [blk, const((FN_GROUP, 2 * FN_GROUP)), const((FN_GROUP, 2 * FN_GROUP))]
    in_specs += [const((seq_len, seq_len))] * 4
    in_specs += [pl.BlockSpec((None, None, FN_GROUP, FN_GROUP), lambda b, g: (o, g, 0, 0))]
    args = [p, cs_hi, cs_lo, *seq_consts, fn_lin]
    aliases = {}
    if mix is not None:
        in_specs.append(pl.BlockSpec(memory_space=pl.ANY))
        args.append(mix)
        aliases = {len(args) - 1: 0}
    return pl.pallas_call(
        _fnet_kernel,
        grid=(n_seq, FN_GROUPS),
        in_specs=in_specs,
        out_specs=blk,
        out_shape=jax.ShapeDtypeStruct((T, D), F32),
        input_output_aliases=aliases,
        compiler_params=_cparams(2),
        name="fnet",
    )(*args)


def _pool_kernel(u_ref, band_ref, cnt_ref, lin_ref, scale_ref, mix_ref, o_ref):
    del mix_ref
    u = u_ref[...]
    band = band_ref[...]
    u_hi, u_lo = _split_bf16(u)
    pooled = (_dot(band, u_hi) + _dot(band, u_lo)) / cnt_ref[...] - u
    o_ref[...] = _dot(pooled.astype(BF16), lin_ref[...].astype(BF16)) * scale_ref[...]


def _pool(p, seq_len, row_block0, n_seq, o, pool_lin, pool_scale, mix):
    bands, cnts = _pool_tables(seq_len)
    n_g = len(POOL_WINDOWS)
    col0 = FN_WIDTH // POOL_GROUP
    blk = pl.BlockSpec((seq_len, POOL_GROUP), lambda b, g: (row_block0 + b, col0 + g))
    return pl.pallas_call(
        _pool_kernel,
        grid=(n_seq, n_g),
        in_specs=[
            blk,
            pl.BlockSpec((None, seq_len, seq_len), lambda b, g: (g, 0, 0)),
            pl.BlockSpec((None, seq_len, 1), lambda b, g: (g, 0, 0)),
            pl.BlockSpec((None, None, POOL_GROUP, POOL_GROUP), lambda b, g: (o, g, 0, 0)),
            pl.BlockSpec((None, 1, POOL_GROUP), lambda b, g: (o, 0, g)),
            pl.BlockSpec(memory_space=pl.ANY),
        ],
        out_specs=blk,
        out_shape=jax.ShapeDtypeStruct((T, D), F32),
        input_output_aliases={5: 0},
        compiler_params=_cparams(2),
        name="pool",
    )(p, bands, cnts, pool_lin, pool_scale.reshape(pool_scale.shape[0], 1, POOL_WIDTH), mix)


def _first_index(values, target):
    idx = jnp.full(target.shape, len(values) - 1, jnp.int32)
    for j in range(len(values) - 2, -1, -1):
        idx = jnp.where(values[j] == target, j, idx)
    return idx


def _select_row(rows, idx):
    out = rows[-1]
    for j in range(len(rows) - 2, -1, -1):
        out = jnp.where(idx == j, rows[j], out)
    return out


def _router_kernel(x_ref, g_ref, sc_ref, sh_ref, wr_ref, br_ref, h_ref, ids_ref, wts_ref, lt_scr, *, tm):
    i = pl.program_id(0)
    row = _mod_row(i * tm)
    scale1 = 1.0 + sc_ref[pl.ds(row, 1), :]
    shift = sh_ref[pl.ds(row, 1), :]
    g = g_ref[...]

    def body(c, carry):
        r0 = pl.multiple_of(c * ROW_CHUNK, ROW_CHUNK)
        h_ref[pl.ds(r0, ROW_CHUNK), :] = _norm_rows(x_ref[pl.ds(r0, ROW_CHUNK), :], g, scale1, shift)
        return carry

    lax.fori_loop(0, tm // ROW_CHUNK, body, 0)

    h_hi, h_lo = _split_bf16(h_ref[...])
    w_hi, w_lo = _split_bf16(wr_ref[...])
    lt_scr[...] = _dot_nt(w_hi, h_hi) + _dot_nt(w_hi, h_lo) + _dot_nt(w_lo, h_hi) + br_ref[...]

    def logit(r):
        return lt_scr[pl.ds(r, 1), :]

    lg = [logit(r) for r in range(MOE_GROUPS)]
    m = functools.reduce(jnp.maximum, lg)
    ex = [jnp.exp(v - m) for v in lg]
    denom = functools.reduce(jnp.add, ex)
    probs = [v / denom for v in ex]
    g_p = functools.reduce(jnp.maximum, probs)
    g_idx = _first_index(probs, g_p)
    sel = [_select_row([logit(MOE_GROUPS + gi * MOE_EPG + j) for gi in range(MOE_GROUPS)], g_idx)
           for j in range(MOE_EPG)]
    v1 = functools.reduce(jnp.maximum, sel)
    i1 = _first_index(sel, v1)
    rest = [jnp.where(i1 == j, -jnp.inf, sel[j]) for j in range(MOE_EPG)]
    v2 = functools.reduce(jnp.maximum, rest)
    i2 = _first_index(rest, v2)
    b = jnp.exp(v2 - v1)
    w1 = g_p / (1.0 + b)
    w2 = g_p * b / (1.0 + b)
    ids_ref[...] = jnp.zeros_like(ids_ref)
    wts_ref[...] = jnp.zeros_like(wts_ref)
    ids_ref[0:1, :] = g_idx * MOE_EPG + i1
    ids_ref[1:2, :] = g_idx * MOE_EPG + i2
    wts_ref[0:1, :] = w1
    wts_ref[1:2, :] = w2


def _router(x, g, scale, shift, layer, w_group, b_group, w_expert, b_expert, *, tm=512):
    pad = ROUTER_ROWS - MOE_GROUPS - MOE_EXPERTS
    wr = jnp.pad(jnp.concatenate([w_group[layer].T, w_expert[layer].T], axis=0), ((0, pad), (0, 0)))
    br = jnp.pad(jnp.concatenate([b_group[layer], b_expert[layer]]), (0, pad)).reshape(ROUTER_ROWS, 1)
    return pl.pallas_call(
        functools.partial(_router_kernel, tm=tm),
        grid=(T // tm,),
        in_specs=[
            pl.BlockSpec((tm, D), lambda i: (i, 0)),
            pl.BlockSpec((None, 1, D), lambda i: (layer, 0, 0)),
            pl.BlockSpec((N_MOD_ROWS, D), lambda i: (0, 0)),
            pl.BlockSpec((N_MOD_ROWS, D), lambda i: (0, 0)),
            pl.BlockSpec((ROUTER_ROWS, D), lambda i: (0, 0)),
            pl.BlockSpec((ROUTER_ROWS, 1), lambda i: (0, 0)),
        ],
        out_specs=[
            pl.BlockSpec((tm, D), lambda i: (i, 0)),
            pl.BlockSpec((8, tm), lambda i: (0, i)),
            pl.BlockSpec((8, tm), lambda i: (0, i)),
        ],
        out_shape=[
            jax.ShapeDtypeStruct((T, D), F32),
            jax.ShapeDtypeStruct((8, T), jnp.int32),
            jax.ShapeDtypeStruct((8, T), F32),
        ],
        scratch_shapes=[pltpu.VMEM((ROUTER_ROWS, tm), F32)],
        compiler_params=_cparams(1),
        name="router",
    )(x, g.reshape(g.shape[0], 1, D), scale, shift, wr, br)


def _dispatch_plan(ids, wts):
    n_pairs = 2 * T
    e_flat = ids[:2].T.reshape(n_pairs)
    w_flat = wts[:2].T.reshape(n_pairs)
    order = jnp.argsort(e_flat, stable=True).astype(jnp.int32)
    counts = jnp.sum(e_flat[:, None] == jnp.arange(MOE_EXPERTS, dtype=jnp.int32)[None, :], axis=0, dtype=jnp.int32)
    padded = (counts + MOE_TM - 1) // MOE_TM * MOE_TM
    pad_end = jnp.cumsum(padded)
    pad_start = pad_end - padded
    raw_start = jnp.cumsum(counts) - counts
    tile_start = jnp.arange(MOE_TILES, dtype=jnp.int32) * MOE_TM
    tile_e = jnp.minimum(jnp.searchsorted(pad_end, tile_start, side="right").astype(jnp.int32), MOE_EXPERTS - 1)
    n_valid = jnp.clip(counts[tile_e] - (tile_start - pad_start[tile_e]), 0, MOE_TM).astype(jnp.int32)
    slot = jnp.arange(MOE_SLOTS, dtype=jnp.int32)
    slot_e = tile_e[slot // MOE_TM]
    rank = slot - pad_start[slot_e]
    valid = (rank >= 0) & (rank < counts[slot_e])
    pair = order[jnp.clip(raw_start[slot_e] + rank, 0, n_pairs - 1)]
    src_tok = jnp.where(valid, pair // 2, 0).astype(jnp.int32)
    dst_row = jnp.where(valid, pair, 0).astype(jnp.int32)
    slot_w = jnp.where(valid, w_flat[pair], 0.0).reshape(MOE_SLOTS, 1)
    return tile_e, n_valid, src_tok, dst_row, slot_w


def _gmm_kernel(te_ref, nv_ref, src_ref, dst_ref, h_hbm, w1_ref, w3_ref, w2_ref, sw_ref, y_hbm,
                xbuf, obuf, w1b, w3b, w2b, sems):
    i = pl.program_id(0)
    n_valid = nv_ref[i]
    base = i * MOE_TM

    def gather_copy(r):
        return pltpu.make_async_copy(h_hbm.at[pl.ds(src_ref[base + r], 1), :], xbuf.at[pl.ds(r, 1), :], sems.at[0])

    def scatter_copy(r):
        return pltpu.make_async_copy(obuf.at[pl.ds(r, 1), :], y_hbm.at[pl.ds(dst_ref[base + r], 1), :], sems.at[1])

    @pl.when(n_valid > 0)
    def _():
        def issue_gather(r, carry):
            gather_copy(r).start()
            return carry

        lax.fori_loop(0, MOE_TM, issue_gather, 0)

        new_expert = jnp.logical_or(i == 0, te_ref[i] != te_ref[jnp.maximum(i - 1, 0)])

        @pl.when(new_expert)
        def _():
            w1b[...] = w1_ref[...].astype(BF16)
            w3b[...] = w3_ref[...].astype(BF16)
            w2b[...] = w2_ref[...].astype(BF16)

        pltpu.make_async_copy(h_hbm.at[pl.ds(0, MOE_TM), :], xbuf, sems.at[0]).wait()
        x = xbuf[...].astype(BF16)
        a = _dot(x, w1b[...])
        hid = _silu(a) * _dot(x, w3b[...]) * sw_ref[...]
        obuf[...] = _dot(hid.astype(BF16), w2b[...])

        def issue_scatter(r, carry):
            scatter_copy(r).start()
            return carry

        lax.fori_loop(0, n_valid, issue_scatter, 0)
        n_aligned = pl.multiple_of(n_valid // 8 * 8, 8)

        @pl.when(n_aligned > 0)
        def _():
            pltpu.make_async_copy(obuf.at[pl.ds(0, n_aligned), :], y_hbm.at[pl.ds(0, n_aligned), :],
                                  sems.at[1]).wait()

        def wait_scatter(r, carry):
            scatter_copy(r).wait()
            return carry

        lax.fori_loop(n_aligned, n_valid, wait_scatter, 0)


def _moe_experts(h, plan, layer, w1, w3, w2):
    tile_e, n_valid, src_tok, dst_row, slot_w = plan
    w_map = lambda i, te, nv, src, dst: (layer, te[i], 0, 0)
    grid_spec = pltpu.PrefetchScalarGridSpec(
        num_scalar_prefetch=4,
        grid=(MOE_TILES,),
        in_specs=[
            pl.BlockSpec(memory_space=pl.ANY),
            pl.BlockSpec((None, None, D, MOE_FF), w_map),
            pl.BlockSpec((None, None, D, MOE_FF), w_map),
            pl.BlockSpec((None, None, MOE_FF, D), w_map),
            pl.BlockSpec((MOE_TM, 1), lambda i, te, nv, src, dst: (i, 0)),
        ],
        out_specs=pl.BlockSpec(memory_space=pl.ANY),
        scratch_shapes=[
            pltpu.VMEM((MOE_TM, D), F32),
            pltpu.VMEM((MOE_TM, D), F32),
            pltpu.VMEM((D, MOE_FF), BF16),
            pltpu.VMEM((D, MOE_FF), BF16),
            pltpu.VMEM((MOE_FF, D), BF16),
            pltpu.SemaphoreType.DMA((2,)),
        ],
    )
    return pl.pallas_call(
        _gmm_kernel,
        grid_spec=grid_spec,
        out_shape=jax.ShapeDtypeStruct((2 * T, D), F32),
        compiler_params=_cparams(1),
        name="moe_experts",
    )(tile_e, n_valid, src_tok, dst_row, h, w1, w3, w2, slot_w)


def _combine_kernel(x_ref, y_ref, gate_ref, o_ref, *, tm):
    gate = gate_ref[pl.ds(_mod_row(pl.program_id(0) * tm), 1), :]

    def body(c, carry):
        r0 = pl.multiple_of(c * ROW_CHUNK, ROW_CHUNK)
        rows = pl.ds(r0, ROW_CHUNK)
        o_ref[rows, :] = x_ref[rows, :] + gate * (y_ref[rows, 0:D] + y_ref[rows, D:2 * D])
        return carry

    lax.fori_loop(0, tm // ROW_CHUNK, body, 0)


def _moe_combine(x, y, gate, *, tm=512):
    return pl.pallas_call(
        functools.partial(_combine_kernel, tm=tm),
        grid=(T // tm,),
        in_specs=[
            pl.BlockSpec((tm, D), lambda i: (i, 0)),
            pl.BlockSpec((tm, 2 * D), lambda i: (i, 0)),
            pl.BlockSpec((N_MOD_ROWS, D), lambda i: (0, 0)),
        ],
        out_specs=pl.BlockSpec((tm, D), lambda i: (i, 0)),
        out_shape=jax.ShapeDtypeStruct((T, D), F32),
        compiler_params=_cparams(1),
        name="moe_combine",
    )(x, y.reshape(T, 2 * D), gate)


def _hier_moe(x, layer, norm_ffn, mods, moe):
    w_group, b_group, w_expert, b_expert, w1, w3, w2 = moe
    h, ids, wts = _router(x, norm_ffn, mods[4], mods[3], layer, w_group, b_group, w_expert, b_expert)
    y = _moe_experts(h, _dispatch_plan(ids, wts), layer, w1, w3, w2)
    return _moe_combine(x, y, mods[5])


def kernel(x_prompt, x_sample, c, cache_k, cache_v, c_ctx, ada_w, ada_b, norm_mix, norm_ffn, norm_final, w_in_even, w_out_even, na_rpb, hy_short_w, hy_short_b, hy_filt_w1, hy_filt_b1, hy_filt_freq, hy_filt_w2, hy_filt_b2, hy_filt_w3, hy_bias_d, w_in_odd, fn_lin, pool_lin, pool_scale, w_out_odd, moe_w_group, moe_b_group, moe_w_expert, moe_b_expert, moe_w1, moe_w3, moe_w2):
    depth = ada_w.shape[0]
    x = jnp.concatenate([x_prompt.reshape(TP, D), x_sample.reshape(TS, D)], axis=0)
    cvec = jnp.concatenate([c_ctx[None, :], c, jnp.zeros((N_MOD_ROWS - 1 - DEC_BATCH, D), F32)], axis=0)
    mod_all = _modulation(cvec.T, ada_w, ada_b)
    moe = (moe_w_group, moe_b_group, moe_w_expert, moe_b_expert, moe_w1, moe_w3, moe_w2)
    new_k, new_v = [], []
    prompt_blocks = TP // DEC_SEQ
    for l in range(depth):
        mods = [mod_all[l, :, k * D:(k + 1) * D] for k in range(6)]
        if l % 2 == 0:
            e = l // 2
            qkvz = _norm_matmul(x, norm_mix, l, mods[1], mods[0], w_in_even, e)
            new_k.append(qkvz[:TP, NA_WIDTH:2 * NA_WIDTH].reshape(BATCH, SEQ, NA_HEADS, NA_HEAD_DIM))
            new_v.append(qkvz[:TP, 2 * NA_WIDTH:3 * NA_WIDTH].reshape(BATCH, SEQ, NA_HEADS, NA_HEAD_DIM))
            mix = _ctx_attention(qkvz)
            mix = _na_attention(qkvz, cache_k[:, e], cache_v[:, e], na_rpb[e], mix)
            filt_w = (hy_filt_w1, hy_filt_b1, hy_filt_freq, hy_filt_w2, hy_filt_b2, hy_filt_w3)
            filt_p = _hyena_filter(SEQ, e, *filt_w)
            filt_s = _hyena_filter(DEC_SEQ, e, *filt_w)
            mix = _hyena(qkvz, SEQ, 0, BATCH, HY_WIDTH, e, hy_short_w, hy_short_b, hy_bias_d, filt_p, mix)
            mix = _hyena(qkvz, DEC_SEQ, prompt_blocks, DEC_BATCH, 256, e, hy_short_w, hy_short_b, hy_bias_d,
                         filt_s, mix)
            x = _proj_residual(mix, w_out_even, e, x, mods[2])
        else:
            o = l // 2
            p = _norm_matmul(x, norm_mix, l, mods[1], mods[0], w_in_odd, o)
            mix = _fnet(p, SEQ, 0, BATCH, o, fn_lin, None)
            mix = _fnet(p, DEC_SEQ, prompt_blocks, DEC_BATCH, o, fn_lin, mix)
            mix = _pool(p, SEQ, 0, BATCH, o, pool_lin, pool_scale, mix)
            mix = _pool(p, DEC_SEQ, prompt_blocks, DEC_BATCH, o, pool_lin, pool_scale, mix)
            x = _proj_residual(mix, w_out_odd, o, x, mods[2])
        x = _hier_moe(x, l, norm_ffn, mods, moe)
    y_prompt = _final_norm(x, norm_final, 0, TP).reshape(BATCH, SEQ, D)
    y_sample = _final_norm(x, norm_final, TP, TS).reshape(DEC_BATCH, DEC_SEQ, D)
    return (y_prompt, y_sample, jnp.stack(new_k, axis=1), jnp.stack(new_v, axis=1))
```
